```python
import jax, jax.numpy as jnp
from jax import lax
import numpy as np

D_MODEL = 1024
BATCH = 16
SEQ = 2048
DEPTH = 2
DEC_BATCH = 8
DEC_SEQ = 64
PAST_LEN = 4096

CHUNK = 64
W_A = D_MODEL
W_B = D_MODEL
W_C = D_MODEL
CONV_A = 3
CONV_B = 31
MLP_CHUNK = 128
C_GROUPS = 8
C_GROUP_DIM = W_C // C_GROUPS
N_BRANCH = 3
EPS = 1e-6
SPLITS = [W_A, W_A, W_A, W_A, 2 * W_B, W_B, W_C, W_C, W_C, D_MODEL, D_MODEL, D_MODEL]
PROJ_DIM = sum(SPLITS)
SPLIT_IDX = [int(i) for i in np.cumsum(SPLITS)[:-1]]

kernel_name = "parallel_conv_gmlp_stream_step"


def rmsnorm(x, g):
    xf = x.astype(jnp.float32)
    y = xf * lax.rsqrt(jnp.mean(xf * xf, axis=-1, keepdims=True) + EPS)
    return (y * g.astype(jnp.float32)).astype(x.dtype)


def layernorm(x, g, b):
    xf = x.astype(jnp.float32)
    mu = jnp.mean(xf, axis=-1, keepdims=True)
    xc = xf - mu
    y = xc * lax.rsqrt(jnp.mean(xc * xc, axis=-1, keepdims=True) + EPS)
    return (y * g.astype(jnp.float32) + b.astype(jnp.float32)).astype(x.dtype)


def causal_dwconv(prev, x, w):
    xp = jnp.concatenate([prev.astype(x.dtype), x], axis=1)
    y = lax.conv_general_dilated(
        xp, w[:, None, :].astype(x.dtype), window_strides=(1,), padding="VALID",
        dimension_numbers=("NWC", "WIO", "NWC"), feature_group_count=x.shape[-1])
    return y, xp[:, -(w.shape[0] - 1):]


def chunk_spatial_mix(v, w_s, b_s):
    bn, t, _ = v.shape
    pad = (-t) % MLP_CHUNK
    vp = jnp.pad(v, ((0, 0), (0, pad), (0, 0)))
    n = (t + pad) // MLP_CHUNK
    vp = vp.reshape(bn, n, MLP_CHUNK, C_GROUPS, C_GROUP_DIM)
    mask = jnp.tril(jnp.ones((MLP_CHUNK, MLP_CHUNK), dtype=bool))
    wm = jnp.where(mask, w_s, jnp.zeros_like(w_s)).astype(v.dtype)
    s = jnp.einsum("gpq,bnqgc->bnpgc", wm, vp) + b_s.T[:, :, None].astype(v.dtype)
    return s.reshape(bn, n * MLP_CHUNK, W_C)[:, :t]


def mixer_layer(x, prev_a, prev_b, norm_g, w_in, conv_a_w, conv_b_w, conv_b_b,
                ln_b_g, ln_b_b, ln_c_g, ln_c_b, w_s, b_s, gate_b, w_branch, w_out):
    h = rmsnorm(x, norm_g)
    proj = jnp.einsum("btd,de->bte", h, w_in.astype(h.dtype))
    (a_b, a_c, a_x, a_z, b_in, b_z, c_u, c_v, c_z, g_a, g_b, g_c) = jnp.split(proj, SPLIT_IDX, axis=-1)
    conv_a_out, new_a = causal_dwconv(prev_a, a_c * a_x, conv_a_w)
    y_a = a_b * conv_a_out * jax.nn.silu(a_z)
    glu = b_in[..., :W_B] * jax.nn.sigmoid(b_in[..., W_B:])
    conv_b_out, new_b = causal_dwconv(prev_b, glu, conv_b_w)
    y_b = jax.nn.silu(layernorm(conv_b_out + conv_b_b.astype(x.dtype), ln_b_g, ln_b_b)) * jax.nn.silu(b_z)
    u = jax.nn.gelu(c_u)
    v = layernorm(jax.nn.gelu(c_v), ln_c_g, ln_c_b)
    y_c = u * chunk_spatial_mix(v, w_s, b_s) * jax.nn.silu(c_z)
    gb = gate_b.astype(x.dtype)
    m = (jax.nn.sigmoid(g_a + gb[0]) * (y_a @ w_branch[0].astype(x.dtype))
         + jax.nn.sigmoid(g_b + gb[1]) * (y_b @ w_branch[1].astype(x.dtype))
         + jax.nn.sigmoid(g_c + gb[2]) * (y_c @ w_branch[2].astype(x.dtype)))
    x = x + m @ w_out.astype(x.dtype)
    return x, new_a, new_b, v


def setup_inputs(seed: int = 0) -> dict:
    key = jax.random.key(seed)
    ks = jax.random.split(key, 20)
    nrm = lambda k, s: jax.random.normal(k, s, jnp.float32)
    return {
        "x_prompt": nrm(ks[0], (BATCH, SEQ, D_MODEL)),
        "x_sample": nrm(ks[1], (DEC_BATCH, DEC_SEQ, D_MODEL)),
        "cache_conv_a": nrm(ks[2], (DEPTH, DEC_BATCH, CONV_A - 1, W_A)),
        "cache_conv_b": nrm(ks[3], (DEPTH, DEC_BATCH, CONV_B - 1, W_B)),
        "norm_g": 1.0 + 0.05 * nrm(ks[4], (DEPTH, D_MODEL)),
        "w_in": nrm(ks[5], (DEPTH, D_MODEL, PROJ_DIM)) * D_MODEL ** -0.5,
        "conv_a_w": nrm(ks[6], (DEPTH, CONV_A, W_A)) * CONV_A ** -0.5,
        "conv_b_w": nrm(ks[7], (DEPTH, CONV_B, W_B)) * CONV_B ** -0.5,
        "conv_b_b": 0.02 * nrm(ks[8], (DEPTH, W_B)),
        "ln_b_g": 1.0 + 0.05 * nrm(ks[9], (DEPTH, W_B)),
        "ln_b_b": 0.02 * nrm(ks[10], (DEPTH, W_B)),
        "ln_c_g": 1.0 + 0.05 * nrm(ks[11], (DEPTH, W_C)),
        "ln_c_b": 0.02 * nrm(ks[12], (DEPTH, W_C)),
        "w_s": nrm(ks[13], (DEPTH, C_GROUPS, MLP_CHUNK, MLP_CHUNK)) * MLP_CHUNK ** -0.5,
        "b_s": 1.0 + 0.1 * nrm(ks[14], (DEPTH, C_GROUPS, MLP_CHUNK)),
        "gate_b": 0.02 * nrm(ks[15], (DEPTH, N_BRANCH, D_MODEL)),
        "w_branch": nrm(ks[16], (DEPTH, N_BRANCH, W_A, D_MODEL)) * W_A ** -0.5,
        "w_out": nrm(ks[17], (DEPTH, D_MODEL, D_MODEL)) * D_MODEL ** -0.5,
        "final_g": 1.0 + 0.05 * nrm(ks[18], (D_MODEL,)),
    }


def reference(x_prompt, x_sample, cache_conv_a, cache_conv_b, norm_g, w_in, conv_a_w, conv_b_w,
              conv_b_b, ln_b_g, ln_b_b, ln_c_g, ln_c_b, w_s, b_s, gate_b, w_branch, w_out, final_g):
    xp = x_prompt
    pa_list, pb_list = [], []
    for l in range(DEPTH):
        zero_a = jnp.zeros((xp.shape[0], CONV_A - 1, W_A), xp.dtype)
        zero_b = jnp.zeros((xp.shape[0], CONV_B - 1, W_B), xp.dtype)
        xp, na, nb, _ = mixer_layer(xp, zero_a, zero_b, norm_g[l], w_in[l], conv_a_w[l], conv_b_w[l],
                                    conv_b_b[l], ln_b_g[l], ln_b_b[l], ln_c_g[l], ln_c_b[l], w_s[l],
                                    b_s[l], gate_b[l], w_branch[l], w_out[l])
        pa_list.append(na)
        pb_list.append(nb)
    y_prompt = rmsnorm(xp, final_g)
    xs = x_sample
    sa_list, sb_list, sv_list = [], [], []
    for l in range(DEPTH):
        xs, na, nb, v = mixer_layer(xs, cache_conv_a[l], cache_conv_b[l], norm_g[l], w_in[l], conv_a_w[l],
                                    conv_b_w[l], conv_b_b[l], ln_b_g[l], ln_b_b[l], ln_c_g[l], ln_c_b[l],
                                    w_s[l], b_s[l], gate_b[l], w_branch[l], w_out[l])
        sa_list.append(na)
        sb_list.append(nb)
        sv_list.append(v)
    y_sample = rmsnorm(xs, final_g)
    new_conv_a_prompt = jnp.stack(pa_list, axis=0)
    new_conv_b_prompt = jnp.stack(pb_list, axis=0)
    new_conv_a_sample = jnp.stack(sa_list, axis=0)
    new_conv_b_sample = jnp.stack(sb_list, axis=0)
    new_chunk_v_sample = jnp.stack(sv_list, axis=0)
    return (y_prompt, y_sample, new_conv_a_prompt, new_conv_b_prompt, new_conv_a_sample, new_conv_b_sample, new_chunk_v_sample)
```

```python
import functools
import math

import jax
import jax.numpy as jnp
from jax import lax
from jax.experimental import pallas as pl
from jax.experimental.pallas import tpu as pltpu

D_MODEL = 1024
LANES = 128
NSLAB = D_MODEL // LANES
CONV_A = 3
CONV_B = 31
HALO_A = 8
HALO_B = 32
MLP_CHUNK = 128
C_GROUPS = 8
EPS = 1e-6
PROJ_DIM = 13 * D_MODEL
OFF_A = 0
OFF_B = 4 * D_MODEL
OFF_C = 7 * D_MODEL
OFF_G = 10 * D_MODEL
VMEM_LIMIT_BYTES = 60000 * 1024

_GELU_C1 = math.sqrt(2.0 / math.pi)
_GELU_C2 = _GELU_C1 * 0.044715


def _sigmoid(x):
    return 0.5 * jnp.tanh(0.5 * x) + 0.5


def _silu(x):
    hx = 0.5 * x
    return hx * jnp.tanh(hx) + hx


def _gelu_tanh(x):
    hx = 0.5 * x
    return hx * jnp.tanh(x * (_GELU_C1 + _GELU_C2 * (x * x))) + hx


def _row_stats_normalize(pieces):
    tot = pieces[0]
    for p in pieces[1:]:
        tot = tot + p
    mu = jnp.sum(tot, axis=-1, keepdims=True) * (1.0 / D_MODEL)
    cen = [p - mu for p in pieces]
    sq = cen[0] * cen[0]
    for c in cen[1:]:
        sq = sq + c * c
    var = jnp.sum(sq, axis=-1, keepdims=True) * (1.0 / D_MODEL)
    rs = lax.rsqrt(var + EPS)
    return [c * rs for c in cen]


def _strips(n_rows, strip, body):
    def step(i, carry):
        body(pl.multiple_of(i * strip, strip))
        return carry
    lax.fori_loop(0, n_rows // strip, step, 0)


def _layer_kernel(x_ref, pa_ref, pb_ref, ng_ref, win_ref, caw_ref, cbw_ref, cbb_ref,
                  lbg_ref, lbb_ref, lcg_ref, lcb_ref, ws_ref, bsb_ref, gb_ref, wbr_ref,
                  wout_ref, fg_ref, *rest, tt, chunk, emit_v, final_norm):
    if emit_v:
        out_ref, na_ref, nb_ref, v_ref = rest[:4]
        scratch = rest[4:]
    else:
        out_ref, na_ref, nb_ref = rest[:3]
        v_ref = None
        scratch = rest[3:]
    hbf, pbuf, gt, qbuf, macc, gbuf, cabuf, ybf, vbf = scratch
    t = pl.program_id(1)
    f32 = jnp.float32
    bf16 = jnp.bfloat16

    def slab(j, off=0):
        return slice(off + LANES * j, off + LANES * (j + 1))

    @pl.when(t == 0)
    def _():
        for j in range(NSLAB):
            gbuf[j, 0:HALO_B, :] = pb_ref[:, slab(j)]
            cabuf[j, 0:HALO_A, :] = pa_ref[:, slab(j)]

    def norm_body(r):
        x = x_ref[pl.ds(r, 16), :]
        ms = jnp.mean(x * x, axis=-1, keepdims=True)
        h = (x * lax.rsqrt(ms + EPS)) * ng_ref[...]
        hbf[pl.ds(r, 16), :] = h.astype(bf16)
    _strips(tt, 16, norm_body)

    def project(dst, dst_off, src_off, n_blocks):
        for c in range(n_blocks):
            dst[:, dst_off + c * D_MODEL:dst_off + (c + 1) * D_MODEL] = jnp.dot(
                hbf[...], win_ref[:, src_off + c * D_MODEL:src_off + (c + 1) * D_MODEL],
                preferred_element_type=f32)

    project(gt, 0, OFF_G, 3)

    def merge(k, first, last):
        qbuf[...] = jnp.dot(ybf[...], wbr_ref[k], preferred_element_type=f32)

        def body(r):
            rows = pl.ds(r, 16)
            for j in range(NSLAB):
                gate = _sigmoid(gt[rows, slab(j, k * D_MODEL)] + gb_ref[k:k + 1, slab(j)])
                val = gate * qbuf[rows, slab(j)]
                if not first:
                    val = val + macc[rows, slab(j)]
                if last:
                    ybf[rows, slab(j)] = val.astype(bf16)
                else:
                    macc[rows, slab(j)] = val
        _strips(tt, 16, body)

    project(pbuf, 0, OFF_A, 4)

    def a_body(r):
        rows = pl.ds(r, 16)
        for j in range(NSLAB):
            ca = pbuf[rows, slab(j, D_MODEL)] * pbuf[rows, slab(j, 2 * D_MODEL)]
            cabuf[j, pl.ds(HALO_A + r, 16), :] = ca
            acc = caw_ref[CONV_A - 1:CONV_A, slab(j)] * ca
            for k in range(CONV_A - 1):
                win = cabuf[j, pl.ds(r + HALO_A - (CONV_A - 1) + k, 16), :]
                acc = acc + caw_ref[k:k + 1, slab(j)] * win
            y = pbuf[rows, slab(j)] * acc * _silu(pbuf[rows, slab(j, 3 * D_MODEL)])
            ybf[rows, slab(j)] = y.astype(bf16)
    _strips(tt, 16, a_body)
    merge(0, first=True, last=False)

    project(pbuf, 0, OFF_B, 3)

    def glu_body(r):
        rows = pl.ds(r, 16)
        for j in range(NSLAB):
            g = pbuf[rows, slab(j)] * _sigmoid(pbuf[rows, slab(j, D_MODEL)])
            gbuf[j, pl.ds(HALO_B + r, 16), :] = g
    _strips(tt, 16, glu_body)

    def b_body(r):
        rows = pl.ds(r, 32)
        convs = []
        for j in range(NSLAB):
            acc = jnp.broadcast_to(cbb_ref[:, slab(j)], (32, LANES))
            for k in range(CONV_B):
                win = gbuf[j, pl.ds(r + HALO_B - (CONV_B - 1) + k, 32), :]
                acc = acc + cbw_ref[k:k + 1, slab(j)] * win
            convs.append(acc)
        normed = _row_stats_normalize(convs)
        for j in range(NSLAB):
            yn = normed[j] * lbg_ref[:, slab(j)] + lbb_ref[:, slab(j)]
            y = _silu(yn) * _silu(pbuf[rows, slab(j, 2 * D_MODEL)])
            ybf[rows, slab(j)] = y.astype(bf16)
    _strips(tt, 32, b_body)
    merge(1, first=False, last=False)

    project(pbuf, 0, OFF_C, 3)

    def v_body(r):
        rows = pl.ds(r, 16)
        gv = [_gelu_tanh(pbuf[rows, slab(j, D_MODEL)]) for j in range(NSLAB)]
        normed = _row_stats_normalize(gv)
        for j in range(NSLAB):
            v = normed[j] * lcg_ref[:, slab(j)] + lcb_ref[:, slab(j)]
            vbf[rows, slab(j)] = v.astype(bf16)
            if emit_v:
                v_ref[rows, slab(j)] = v
    _strips(tt, 16, v_body)

    row_id = lax.broadcasted_iota(jnp.int32, (MLP_CHUNK, MLP_CHUNK), 0)
    col_id = lax.broadcasted_iota(jnp.int32, (MLP_CHUNK, MLP_CHUNK), 1)
    lower = row_id >= col_id
    for g in range(C_GROUPS):
        wm = jnp.where(lower, ws_ref[g], 0.0).astype(bf16)[:chunk, :chunk]
        for c in range(tt // chunk):
            rows = slice(c * chunk, (c + 1) * chunk)
            qbuf[rows, slab(g)] = jnp.dot(wm, vbf[rows, slab(g)], preferred_element_type=f32)

    def c_body(r):
        rows = pl.ds(r, 16)
        prow = pl.ds(pl.multiple_of(jnp.bitwise_and(r, chunk - 1), 16), 16)
        for j in range(NSLAB):
            s = qbuf[rows, slab(j)] + bsb_ref[prow, slab(j)]
            y = _gelu_tanh(pbuf[rows, slab(j)]) * s * _silu(pbuf[rows, slab(j, 2 * D_MODEL)])
            ybf[rows, slab(j)] = y.astype(bf16)
    _strips(tt, 16, c_body)
    merge(2, first=False, last=True)

    qbuf[...] = jnp.dot(ybf[...], wout_ref[...], preferred_element_type=f32)

    def out_body(r):
        rows = pl.ds(r, 16)
        y = x_ref[rows, :] + qbuf[rows, :]
        if final_norm:
            ms = jnp.mean(y * y, axis=-1, keepdims=True)
            y = (y * lax.rsqrt(ms + EPS)) * fg_ref[...]
        out_ref[rows, :] = y
    _strips(tt, 16, out_body)

    for j in range(NSLAB):
        tail_b = gbuf[j, tt:tt + HALO_B, :]
        tail_a = cabuf[j, tt:tt + HALO_A, :]
        gbuf[j, 0:HALO_B, :] = tail_b
        cabuf[j, 0:HALO_A, :] = tail_a
        nb_ref[:, slab(j)] = tail_b
        na_ref[:, slab(j)] = tail_a


def _mixer_layer(x, prev_a, prev_b, params, layer, *, tt, emit_v, final_norm):
    bsz, seq, _ = x.shape
    assert seq % tt == 0 and tt % 32 == 0
    chunk = min(tt, MLP_CHUNK)
    assert tt % chunk == 0 and (seq == tt or tt % MLP_CHUNK == 0)
    grid = (bsz, seq // tt)
    once = pl.Buffered(1)

    def per_layer(*block):
        zeros = (0,) * len(block)
        return pl.BlockSpec((None,) + block, lambda b, t: (layer,) + zeros, pipeline_mode=once)

    act = pl.BlockSpec((None, tt, D_MODEL), lambda b, t: (b, t, 0))
    in_specs = [
        act,
        pl.BlockSpec((None, HALO_A, D_MODEL), lambda b, t: (b, 0, 0)),
        pl.BlockSpec((None, HALO_B, D_MODEL), lambda b, t: (b, 0, 0)),
        per_layer(1, D_MODEL),
        per_layer(D_MODEL, PROJ_DIM),
        per_layer(CONV_A, D_MODEL),
        per_layer(CONV_B, D_MODEL),
        per_layer(1, D_MODEL),
        per_layer(1, D_MODEL),
        per_layer(1, D_MODEL),
        per_layer(1, D_MODEL),
        per_layer(1, D_MODEL),
        per_layer(C_GROUPS, MLP_CHUNK, MLP_CHUNK),
        per_layer(MLP_CHUNK, D_MODEL),
        per_layer(3, D_MODEL),
        per_layer(3, D_MODEL, D_MODEL),
        per_layer(D_MODEL, D_MODEL),
        pl.BlockSpec((1, D_MODEL), lambda b, t: (0, 0), pipeline_mode=once),
    ]
    out_shape = [
        jax.ShapeDtypeStruct((bsz, seq, D_MODEL), jnp.float32),
        jax.ShapeDtypeStruct((bsz, HALO_A, D_MODEL), jnp.float32),
        jax.ShapeDtypeStruct((bsz, HALO_B, D_MODEL), jnp.float32),
    ]
    out_specs = [
        act,
        pl.BlockSpec((None, HALO_A, D_MODEL), lambda b, t: (b, 0, 0)),
        pl.BlockSpec((None, HALO_B, D_MODEL), lambda b, t: (b, 0, 0)),
    ]
    if emit_v:
        out_shape.append(jax.ShapeDtypeStruct((bsz, seq, D_MODEL), jnp.float32))
        out_specs.append(act)
    scratch_shapes = [
        pltpu.VMEM((tt, D_MODEL), jnp.bfloat16),
        pltpu.VMEM((tt, 4 * D_MODEL), jnp.float32),
        pltpu.VMEM((tt, 3 * D_MODEL), jnp.float32),
        pltpu.VMEM((tt, D_MODEL), jnp.float32),
        pltpu.VMEM((tt, D_MODEL), jnp.float32),
        pltpu.VMEM((NSLAB, HALO_B + tt, LANES), jnp.float32),
        pltpu.VMEM((NSLAB, HALO_A + tt, LANES), jnp.float32),
        pltpu.VMEM((tt, D_MODEL), jnp.bfloat16),
        pltpu.VMEM((tt, D_MODEL), jnp.bfloat16),
    ]
    kern = functools.partial(_layer_kernel, tt=tt, chunk=chunk, emit_v=emit_v,
                             final_norm=final_norm)
    return pl.pallas_call(
        kern,
        grid=grid,
        in_specs=in_specs,
        out_specs=out_specs,
        out_shape=out_shape,
        scratch_shapes=scratch_shapes,
        compiler_params=pltpu.CompilerParams(
            dimension_semantics=("arbitrary", "arbitrary"),
            vmem_limit_bytes=VMEM_LIMIT_BYTES),
        name=f"mixer_layer{layer}_t{tt}",
    )(x, prev_a, prev_b, *params)


def _right_align(prev, halo):
    return jnp.pad(prev, ((0, 0), (halo - prev.shape[1], 0), (0, 0)))


def kernel(x_prompt, x_sample, cache_conv_a, cache_conv_b, norm_g, w_in, conv_a_w, conv_b_w,
           conv_b_b, ln_b_g, ln_b_b, ln_c_g, ln_c_b, w_s, b_s, gate_b, w_branch, w_out, final_g):
    depth = w_in.shape[0]
    row = lambda p: p.reshape(depth, 1, D_MODEL)
    bsb = jnp.broadcast_to(jnp.swapaxes(b_s, 1, 2)[..., None],
                           (depth, MLP_CHUNK, C_GROUPS, D_MODEL // C_GROUPS))
    params = (row(norm_g), w_in.astype(jnp.bfloat16), conv_a_w, conv_b_w, row(conv_b_b),
              row(ln_b_g), row(ln_b_b), row(ln_c_g), row(ln_c_b), w_s,
              bsb.reshape(depth, MLP_CHUNK, D_MODEL), gate_b, w_branch.astype(jnp.bfloat16),
              w_out.astype(jnp.bfloat16), final_g.reshape(1, D_MODEL))

    def run_group(x, prev_a_layers, prev_b_layers, tt, emit_v):
        new_a, new_b, vs = [], [], []
        for l in range(depth):
            res = _mixer_layer(x, prev_a_layers[l], prev_b_layers[l], params, l, tt=tt,
                               emit_v=emit_v, final_norm=(l == depth - 1))
            x = res[0]
            new_a.append(res[1][:, HALO_A - (CONV_A - 1):])
            new_b.append(res[2][:, HALO_B - (CONV_B - 1):])
            if emit_v:
                vs.append(res[3])
        return x, jnp.stack(new_a), jnp.stack(new_b), vs

    bp = x_prompt.shape[0]
    zero_a = jnp.zeros((bp, HALO_A, D_MODEL), x_prompt.dtype)
    zero_b = jnp.zeros((bp, HALO_B, D_MODEL), x_prompt.dtype)
    y_prompt, pa, pb, _ = run_group(x_prompt, [zero_a] * depth, [zero_b] * depth, 256, False)

    sa = [_right_align(cache_conv_a[l], HALO_A) for l in range(depth)]
    sb = [_right_align(cache_conv_b[l], HALO_B) for l in range(depth)]
    y_sample, na, nb, vs = run_group(x_sample, sa, sb, x_sample.shape[1], True)
    return (y_prompt, y_sample, pa, pb, na, nb, jnp.stack(vs))
```

```python
import functools
import math

import jax
import jax.numpy as jnp
from jax import lax
from jax.experimental import pallas as pl
from jax.experimental.pallas import tpu as pltpu

D_MODEL = 1024
LANES = 128
NSLAB = D_MODEL // LANES
CONV_A = 3
CONV_B = 31
HALO_A = 8
HALO_B = 32
MLP_CHUNK = 128
C_GROUPS = 8
EPS = 1e-6
PROJ_DIM = 13 * D_MODEL
OFF_A = 0
OFF_B = 4 * D_MODEL
OFF_C = 7 * D_MODEL
OFF_G = 10 * D_MODEL
VMEM_LIMIT_BYTES = 60000 * 1024

_GELU_C1 = math.sqrt(2.0 / math.pi)
_GELU_C2 = _GELU_C1 * 0.044715


def _sigmoid(x):
    return 0.5 * jnp.tanh(0.5 * x) + 0.5


def _silu(x):
    hx = 0.5 * x
    return hx * jnp.tanh(hx) + hx


def _gelu_tanh(x):
    hx = 0.5 * x
    return hx * jnp.tanh(x * (_GELU_C1 + _GELU_C2 * (x * x))) + hx


def _row_stats_normalize(pieces):
    tot = pieces[0]
    for p in pieces[1:]:
        tot = tot + p
    mu = jnp.sum(tot, axis=-1, keepdims=True) * (1.0 / D_MODEL)
    cen = [p - mu for p in pieces]
    sq = cen[0] * cen[0]
    for c in cen[1:]:
        sq = sq + c * c
    var = jnp.sum(sq, axis=-1, keepdims=True) * (1.0 / D_MODEL)
    rs = lax.rsqrt(var + EPS)
    return [c * rs for c in cen]


def _strips(n_rows, strip, body):
    for i in range(n_rows // strip):
        body(i * strip)


def _layer_kernel(x_ref, pa_ref, pb_ref, ng_ref, win_ref, caw_ref, cbw_ref, cbb_ref,
                  lbg_ref, lbb_ref, lcg_ref, lcb_ref, ws_ref, bsb_ref, gb_ref, wbr_ref,
                  wout_ref, fg_ref, *rest, tt, chunk, emit_v, final_norm):
    if emit_v:
        out_ref, na_ref, nb_ref, v_ref = rest[:4]
        scratch = rest[4:]
    else:
        out_ref, na_ref, nb_ref = rest[:3]
        v_ref = None
        scratch = rest[3:]
    hbf, pbuf, gt, qbuf, macc, gbuf, cabuf, ybf, vbf = scratch
    t = pl.program_id(1)
    f32 = jnp.float32
    bf16 = jnp.bfloat16

    def slab(j, off=0):
        return slice(off + LANES * j, off + LANES * (j + 1))

    @pl.when(t == 0)
    def _():
        for j in range(NSLAB):
            gbuf[j, 0:HALO_B, :] = pb_ref[:, slab(j)]
            cabuf[j, 0:HALO_A, :] = pa_ref[:, slab(j)]

    def norm_body(r):
        x = x_ref[pl.ds(r, 16), :]
        ms = jnp.mean(x * x, axis=-1, keepdims=True)
        h = (x * lax.rsqrt(ms + EPS)) * ng_ref[...]
        hbf[pl.ds(r, 16), :] = h.astype(bf16)
    _strips(tt, 16, norm_body)

    def project(dst, dst_off, src_off, n_blocks):
        for c in range(n_blocks):
            dst[:, dst_off + c * D_MODEL:dst_off + (c + 1) * D_MODEL] = jnp.dot(
                hbf[...], win_ref[:, src_off + c * D_MODEL:src_off + (c + 1) * D_MODEL],
                preferred_element_type=f32)

    project(gt, 0, OFF_G, 3)

    def merge(k, first, last):
        qbuf[...] = jnp.dot(ybf[...], wbr_ref[k], preferred_element_type=f32)

        def body(r):
            rows = pl.ds(r, 16)
            for j in range(NSLAB):
                gate = _sigmoid(gt[rows, slab(j, k * D_MODEL)] + gb_ref[k:k + 1, slab(j)])
                val = gate * qbuf[rows, slab(j)]
                if not first:
                    val = val + macc[rows, slab(j)]
                if last:
                    ybf[rows, slab(j)] = val.astype(bf16)
                else:
                    macc[rows, slab(j)] = val
        _strips(tt, 16, body)

    project(pbuf, 0, OFF_A, 4)

    def a_body(r):
        rows = pl.ds(r, 16)
        for j in range(NSLAB):
            ca = pbuf[rows, slab(j, D_MODEL)] * pbuf[rows, slab(j, 2 * D_MODEL)]
            cabuf[j, pl.ds(HALO_A + r, 16), :] = ca
            acc = caw_ref[CONV_A - 1:CONV_A, slab(j)] * ca
            for k in range(CONV_A - 1):
                win = cabuf[j, pl.ds(r + HALO_A - (CONV_A - 1) + k, 16), :]
                acc = acc + caw_ref[k:k + 1, slab(j)] * win
            y = pbuf[rows, slab(j)] * acc * _silu(pbuf[rows, slab(j, 3 * D_MODEL)])
            ybf[rows, slab(j)] = y.astype(bf16)
    _strips(tt, 16, a_body)
    merge(0, first=True, last=False)

    project(pbuf, 0, OFF_B, 3)

    def glu_body(r):
        rows = pl.ds(r, 16)
        for j in range(NSLAB):
            g = pbuf[rows, slab(j)] * _sigmoid(pbuf[rows, slab(j, D_MODEL)])
            gbuf[j, pl.ds(HALO_B + r, 16), :] = g
    _strips(tt, 16, glu_body)

    def b_body(r):
        rows = pl.ds(r, 32)
        convs = []
        for j in range(NSLAB):
            acc = jnp.broadcast_to(cbb_ref[:, slab(j)], (32, LANES))
            for k in range(CONV_B):
                win = gbuf[j, pl.ds(r + HALO_B - (CONV_B - 1) + k, 32), :]
                acc = acc + cbw_ref[k:k + 1, slab(j)] * win
            convs.append(acc)
        normed = _row_stats_normalize(convs)
        for j in range(NSLAB):
            yn = normed[j] * lbg_ref[:, slab(j)] + lbb_ref[:, slab(j)]
            y = _silu(yn) * _silu(pbuf[rows, slab(j, 2 * D_MODEL)])
            ybf[rows, slab(j)] = y.astype(bf16)
    _strips(tt, 32, b_body)
    merge(1, first=False, last=False)

    project(pbuf, 0, OFF_C, 3)

    def v_body(r):
        rows = pl.ds(r, 16)
        gv = [_gelu_tanh(pbuf[rows, slab(j, D_MODEL)]) for j in range(NSLAB)]
        normed = _row_stats_normalize(gv)
        for j in range(NSLAB):
            v = normed[j] * lcg_ref[:, slab(j)] + lcb_ref[:, slab(j)]
            vbf[rows, slab(j)] = v.astype(bf16)
            if emit_v:
                v_ref[rows, slab(j)] = v
    _strips(tt, 16, v_body)

    row_id = lax.broadcasted_iota(jnp.int32, (MLP_CHUNK, MLP_CHUNK), 0)
    col_id = lax.broadcasted_iota(jnp.int32, (MLP_CHUNK, MLP_CHUNK), 1)
    lower = row_id >= col_id
    for g in range(C_GROUPS):
        wm = jnp.where(lower, ws_ref[g], 0.0).astype(bf16)[:chunk, :chunk]
        for c in range(tt // chunk):
            rows = slice(c * chunk, (c + 1) * chunk)
            qbuf[rows, slab(g)] = jnp.dot(wm, vbf[rows, slab(g)], preferred_element_type=f32)

    def c_body(r):
        rows = pl.ds(r, 16)
        prow = pl.ds(r % chunk, 16)
        for j in range(NSLAB):
            s = qbuf[rows, slab(j)] + bsb_ref[prow, slab(j)]
            y = _gelu_tanh(pbuf[rows, slab(j)]) * s * _silu(pbuf[rows, slab(j, 2 * D_MODEL)])
            ybf[rows, slab(j)] = y.astype(bf16)
    _strips(tt, 16, c_body)
    merge(2, first=False, last=True)

    qbuf[...] = jnp.dot(ybf[...], wout_ref[...], preferred_element_type=f32)

    def out_body(r):
        rows = pl.ds(r, 16)
        y = x_ref[rows, :] + qbuf[rows, :]
        if final_norm:
            ms = jnp.mean(y * y, axis=-1, keepdims=True)
            y = (y * lax.rsqrt(ms + EPS)) * fg_ref[...]
        out_ref[rows, :] = y
    _strips(tt, 16, out_body)

    for j in range(NSLAB):
        tail_b = gbuf[j, tt:tt + HALO_B, :]
        tail_a = cabuf[j, tt:tt + HALO_A, :]
        gbuf[j, 0:HALO_B, :] = tail_b
        cabuf[j, 0:HALO_A, :] = tail_a
        nb_ref[:, slab(j)] = tail_b
        na_ref[:, slab(j)] = tail_a


def _mixer_layer(x, prev_a, prev_b, params, layer, *, tt, emit_v, final_norm):
    bsz, seq, _ = x.shape
    assert seq % tt == 0 and tt % 32 == 0
    chunk = min(tt, MLP_CHUNK)
    assert tt % chunk == 0 and (seq == tt or tt % MLP_CHUNK == 0)
    grid = (bsz, seq // tt)
    once = pl.Buffered(1)

    def per_layer(*block):
        zeros = (0,) * len(block)
        return pl.BlockSpec((None,) + block, lambda b, t: (layer,) + zeros, pipeline_mode=once)

    act = pl.BlockSpec((None, tt, D_MODEL), lambda b, t: (b, t, 0))
    in_specs = [
        act,
        pl.BlockSpec((None, HALO_A, D_MODEL), lambda b, t: (b, 0, 0)),
        pl.BlockSpec((None, HALO_B, D_MODEL), lambda b, t: (b, 0, 0)),
        per_layer(1, D_MODEL),
        per_layer(D_MODEL, PROJ_DIM),
        per_layer(CONV_A, D_MODEL),
        per_layer(CONV_B, D_MODEL),
        per_layer(1, D_MODEL),
        per_layer(1, D_MODEL),
        per_layer(1, D_MODEL),
        per_layer(1, D_MODEL),
        per_layer(1, D_MODEL),
        per_layer(C_GROUPS, MLP_CHUNK, MLP_CHUNK),
        per_layer(MLP_CHUNK, D_MODEL),
        per_layer(3, D_MODEL),
        per_layer(3, D_MODEL, D_MODEL),
        per_layer(D_MODEL, D_MODEL),
        pl.BlockSpec((1, D_MODEL), lambda b, t: (0, 0), pipeline_mode=once),
    ]
    out_shape = [
        jax.ShapeDtypeStruct((bsz, seq, D_MODEL), jnp.float32),
        jax.ShapeDtypeStruct((bsz, HALO_A, D_MODEL), jnp.float32),
        jax.ShapeDtypeStruct((bsz, HALO_B, D_MODEL), jnp.float32),
    ]
    out_specs = [
        act,
        pl.BlockSpec((None, HALO_A, D_MODEL), lambda b, t: (b, 0, 0)),
        pl.BlockSpec((None, HALO_B, D_MODEL), lambda b, t: (b, 0, 0)),
    ]
    if emit_v:
        out_shape.append(jax.ShapeDtypeStruct((bsz, seq, D_MODEL), jnp.float32))
        out_specs.append(act)
    scratch_shapes = [
        pltpu.VMEM((tt, D_MODEL), jnp.bfloat16),
        pltpu.VMEM((tt, 4 * D_MODEL), jnp.float32),
        pltpu.VMEM((tt, 3 * D_MODEL), jnp.float32),
        pltpu.VMEM((tt, D_MODEL), jnp.float32),
        pltpu.VMEM((tt, D_MODEL), jnp.float32),
        pltpu.VMEM((NSLAB, HALO_B + tt, LANES), jnp.float32),
        pltpu.VMEM((NSLAB, HALO_A + tt, LANES), jnp.float32),
        pltpu.VMEM((tt, D_MODEL), jnp.bfloat16),
        pltpu.VMEM((tt, D_MODEL), jnp.bfloat16),
    ]
    kern = functools.partial(_layer_kernel, tt=tt, chunk=chunk, emit_v=emit_v,
                             final_norm=final_norm)
    return pl.pallas_call(
        kern,
        grid=grid,
        in_specs=in_specs,
        out_specs=out_specs,
        out_shape=out_shape,
        scratch_shapes=scratch_shapes,
        compiler_params=pltpu.CompilerParams(
            dimension_semantics=("arbitrary", "arbitrary"),
            vmem_limit_bytes=VMEM_LIMIT_BYTES),
        name=f"mixer_layer{layer}_t{tt}",
    )(x, prev_a, prev_b, *params)


def _right_align(prev, halo):
    return jnp.pad(prev, ((0, 0), (halo - prev.shape[1], 0), (0, 0)))


def kernel(x_prompt, x_sample, cache_conv_a, cache_conv_b, norm_g, w_in, conv_a_w, conv_b_w,
           conv_b_b, ln_b_g, ln_b_b, ln_c_g, ln_c_b, w_s, b_s, gate_b, w_branch, w_out, final_g):
    depth = w_in.shape[0]
    row = lambda p: p.reshape(depth, 1, D_MODEL)
    bsb = jnp.broadcast_to(jnp.swapaxes(b_s, 1, 2)[..., None],
                           (depth, MLP_CHUNK, C_GROUPS, D_MODEL // C_GROUPS))
    params = (row(norm_g), w_in.astype(jnp.bfloat16), conv_a_w, conv_b_w, row(conv_b_b),
              row(ln_b_g), row(ln_b_b), row(ln_c_g), row(ln_c_b), w_s,
              bsb.reshape(depth, MLP_CHUNK, D_MODEL), gate_b, w_branch.astype(jnp.bfloat16),
              w_out.astype(jnp.bfloat16), final_g.reshape(1, D_MODEL))

    def run_group(x, prev_a_layers, prev_b_layers, tt, emit_v):
        new_a, new_b, vs = [], [], []
        for l in range(depth):
            res = _mixer_layer(x, prev_a_layers[l], prev_b_layers[l], params, l, tt=tt,
                               emit_v=emit_v, final_norm=(l == depth - 1))
            x = res[0]
            new_a.append(res[1][:, HALO_A - (CONV_A - 1):])
            new_b.append(res[2][:, HALO_B - (CONV_B - 1):])
            if emit_v:
                vs.append(res[3])
        return x, jnp.stack(new_a), jnp.stack(new_b), vs

    bp = x_prompt.shape[0]
    zero_a = jnp.zeros((bp, HALO_A, D_MODEL), x_prompt.dtype)
    zero_b = jnp.zeros((bp, HALO_B, D_MODEL), x_prompt.dtype)
    y_prompt, pa, pb, _ = run_group(x_prompt, [zero_a] * depth, [zero_b] * depth, 128, False)

    sa = [_right_align(cache_conv_a[l], HALO_A) for l in range(depth)]
    sb = [_right_align(cache_conv_b[l], HALO_B) for l in range(depth)]
    y_sample, na, nb, vs = run_group(x_sample, sa, sb, x_sample.shape[1], True)
    return (y_prompt, y_sample, pa, pb, na, nb, jnp.stack(vs))
```

```python
import functools
import math

import jax
import jax.numpy as jnp
from jax import lax
from jax.experimental import pallas as pl
from jax.experimental.pallas import tpu as pltpu

D_MODEL = 1024
LANES = 128
NSLAB = D_MODEL // LANES
CONV_A = 3
CONV_B = 31
HALO_A = 8
HALO_B = 32
MLP_CHUNK = 128
C_GROUPS = 8
EPS = 1e-6
PROJ_DIM = 13 * D_MODEL
OFF_A = 0
OFF_B = 4 * D_MODEL
OFF_C = 7 * D_MODEL
OFF_G = 10 * D_MODEL
VMEM_LIMIT_BYTES = 60000 * 1024

_GELU_C1 = math.sqrt(2.0 / math.pi)
_GELU_C2 = _GELU_C1 * 0.044715


def _sigmoid(x):
    return 0.5 * jnp.tanh(0.5 * x) + 0.5


def _silu(x):
    hx = 0.5 * x
    return hx * jnp.tanh(hx) + hx


def _gelu_tanh(x):
    hx = 0.5 * x
    return hx * jnp.tanh(x * (_GELU_C1 + _GELU_C2 * (x * x))) + hx


def _row_stats_normalize(pieces):
    tot = pieces[0]
    for p in pieces[1:]:
        tot = tot + p
    mu = jnp.sum(tot, axis=-1, keepdims=True) * (1.0 / D_MODEL)
    cen = [p - mu for p in pieces]
    sq = cen[0] * cen[0]
    for c in cen[1:]:
        sq = sq + c * c
    var = jnp.sum(sq, axis=-1, keepdims=True) * (1.0 / D_MODEL)
    rs = lax.rsqrt(var + EPS)
    return [c * rs for c in cen]


def _strips(n_rows, strip, body):
    for i in range(n_rows // strip):
        body(i * strip)


def _unpack(w_words):
    return pltpu.bitcast(w_words, jnp.bfloat16)


def _layer_kernel(x_ref, pa_ref, pb_ref, ng_ref, win_ref, caw_ref, cbw_ref, cbb_ref,
                  lbg_ref, lbb_ref, lcg_ref, lcb_ref, ws_ref, bsb_ref, gb_ref, wbr_ref,
                  wout_ref, fg_ref, *rest, tt, chunk, emit_v, final_norm):
    if emit_v:
        out_ref, na_ref, nb_ref, v_ref = rest[:4]
        scratch = rest[4:]
    else:
        out_ref, na_ref, nb_ref = rest[:3]
        v_ref = None
        scratch = rest[3:]
    (hbf, p_a, p_b, p_c, p_g, q_a, q_b, q_c, q_s, q_o, macc, gbuf, cabuf,
     y_a, y_b, y_c, mbf, vbf) = scratch
    t = pl.program_id(1)
    f32 = jnp.float32
    bf16 = jnp.bfloat16

    def slab(j, off=0):
        return slice(off + LANES * j, off + LANES * (j + 1))

    @pl.when(t == 0)
    def _():
        for j in range(NSLAB):
            gbuf[j, 0:HALO_B, :] = pb_ref[:, slab(j)]
            cabuf[j, 0:HALO_A, :] = pa_ref[:, slab(j)]

    def norm_body(r):
        x = x_ref[pl.ds(r, 16), :]
        ms = jnp.mean(x * x, axis=-1, keepdims=True)
        h = (x * lax.rsqrt(ms + EPS)) * ng_ref[...]
        hbf[pl.ds(r, 16), :] = h.astype(bf16)
    _strips(tt, 16, norm_body)

    def project(dst, src_off, n_blocks):
        for c in range(n_blocks):
            cols = slice(src_off + c * D_MODEL, src_off + (c + 1) * D_MODEL)
            dst[:, c * D_MODEL:(c + 1) * D_MODEL] = jnp.dot(
                hbf[...], _unpack(win_ref[:, cols]), preferred_element_type=f32)

    project(p_b, OFF_B, 3)
    project(p_a, OFF_A, 4)
    project(p_c, OFF_C, 3)
    project(p_g, OFF_G, 3)

    def glu_body(r):
        rows = pl.ds(r, 16)
        for j in range(NSLAB):
            g = p_b[rows, slab(j)] * _sigmoid(p_b[rows, slab(j, D_MODEL)])
            gbuf[j, pl.ds(HALO_B + r, 16), :] = g
    _strips(tt, 16, glu_body)

    def b_body(r):
        rows = pl.ds(r, 32)
        convs = []
        for j in range(NSLAB):
            acc = jnp.broadcast_to(cbb_ref[:, slab(j)], (32, LANES))
            for k in range(CONV_B):
                win = gbuf[j, pl.ds(r + HALO_B - (CONV_B - 1) + k, 32), :]
                acc = acc + cbw_ref[k:k + 1, slab(j)] * win
            convs.append(acc)
        normed = _row_stats_normalize(convs)
        for j in range(NSLAB):
            yn = normed[j] * lbg_ref[:, slab(j)] + lbb_ref[:, slab(j)]
            y = _silu(yn) * _silu(p_b[rows, slab(j, 2 * D_MODEL)])
            y_b[rows, slab(j)] = y.astype(bf16)
    _strips(tt, 32, b_body)
    q_b[...] = jnp.dot(y_b[...], _unpack(wbr_ref[1]), preferred_element_type=f32)

    def a_body(r):
        rows = pl.ds(r, 16)
        for j in range(NSLAB):
            ca = p_a[rows, slab(j, D_MODEL)] * p_a[rows, slab(j, 2 * D_MODEL)]
            cabuf[j, pl.ds(HALO_A + r, 16), :] = ca
            acc = caw_ref[CONV_A - 1:CONV_A, slab(j)] * ca
            for k in range(CONV_A - 1):
                win = cabuf[j, pl.ds(r + HALO_A - (CONV_A - 1) + k, 16), :]
                acc = acc + caw_ref[k:k + 1, slab(j)] * win
            y = p_a[rows, slab(j)] * acc * _silu(p_a[rows, slab(j, 3 * D_MODEL)])
            y_a[rows, slab(j)] = y.astype(bf16)
    _strips(tt, 16, a_body)
    q_a[...] = jnp.dot(y_a[...], _unpack(wbr_ref[0]), preferred_element_type=f32)

    def v_body(r):
        rows = pl.ds(r, 16)
        gv = [_gelu_tanh(p_c[rows, slab(j, D_MODEL)]) for j in range(NSLAB)]
        normed = _row_stats_normalize(gv)
        for j in range(NSLAB):
            v = normed[j] * lcg_ref[:, slab(j)] + lcb_ref[:, slab(j)]
            vbf[rows, slab(j)] = v.astype(bf16)
            if emit_v:
                v_ref[rows, slab(j)] = v
    _strips(tt, 16, v_body)

    row_id = lax.broadcasted_iota(jnp.int32, (chunk, chunk), 0)
    col_id = lax.broadcasted_iota(jnp.int32, (chunk, chunk), 1)
    lower = row_id >= col_id
    for g in range(C_GROUPS):
        wm = jnp.where(lower, ws_ref[g, 0:chunk, 0:chunk], 0.0).astype(bf16)
        for c in range(tt // chunk):
            rows = slice(c * chunk, (c + 1) * chunk)
            q_s[rows, slab(g)] = jnp.dot(wm, vbf[rows, slab(g)], preferred_element_type=f32)

    def merge_ab_body(r):
        rows = pl.ds(r, 16)
        for j in range(NSLAB):
            ga = _sigmoid(p_g[rows, slab(j)] + gb_ref[0:1, slab(j)])
            gb = _sigmoid(p_g[rows, slab(j, D_MODEL)] + gb_ref[1:2, slab(j)])
            macc[rows, slab(j)] = ga * q_a[rows, slab(j)] + gb * q_b[rows, slab(j)]
    _strips(tt, 16, merge_ab_body)

    def c_body(r):
        rows = pl.ds(r, 16)
        prow = pl.ds(r % chunk, 16)
        for j in range(NSLAB):
            s = q_s[rows, slab(j)] + bsb_ref[prow, slab(j)]
            y = _gelu_tanh(p_c[rows, slab(j)]) * s * _silu(p_c[rows, slab(j, 2 * D_MODEL)])
            y_c[rows, slab(j)] = y.astype(bf16)
    _strips(tt, 16, c_body)
    q_c[...] = jnp.dot(y_c[...], _unpack(wbr_ref[2]), preferred_element_type=f32)

    def merge_c_body(r):
        rows = pl.ds(r, 16)
        for j in range(NSLAB):
            gc = _sigmoid(p_g[rows, slab(j, 2 * D_MODEL)] + gb_ref[2:3, slab(j)])
            mbf[rows, slab(j)] = (macc[rows, slab(j)] + gc * q_c[rows, slab(j)]).astype(bf16)
    _strips(tt, 16, merge_c_body)

    q_o[...] = jnp.dot(mbf[...], _unpack(wout_ref[...]), preferred_element_type=f32)

    def out_body(r):
        rows = pl.ds(r, 16)
        y = x_ref[rows, :] + q_o[rows, :]
        if final_norm:
            ms = jnp.mean(y * y, axis=-1, keepdims=True)
            y = (y * lax.rsqrt(ms + EPS)) * fg_ref[...]
        out_ref[rows, :] = y
    _strips(tt, 16, out_body)

    for j in range(NSLAB):
        tail_b = gbuf[j, tt:tt + HALO_B, :]
        tail_a = cabuf[j, tt:tt + HALO_A, :]
        gbuf[j, 0:HALO_B, :] = tail_b
        cabuf[j, 0:HALO_A, :] = tail_a
        nb_ref[:, slab(j)] = tail_b
        na_ref[:, slab(j)] = tail_a


def _mixer_layer(x, prev_a, prev_b, params, layer, *, tt, emit_v, final_norm):
    bsz, seq, _ = x.shape
    assert seq % tt == 0 and tt % 32 == 0
    chunk = min(tt, MLP_CHUNK)
    assert tt % chunk == 0 and (seq == tt or tt % MLP_CHUNK == 0)
    grid = (bsz, seq // tt)
    once = pl.Buffered(1)

    def per_layer(*block):
        zeros = (0,) * len(block)
        return pl.BlockSpec((None,) + block, lambda b, t: (layer,) + zeros, pipeline_mode=once)

    act = pl.BlockSpec((None, tt, D_MODEL), lambda b, t: (b, t, 0))
    in_specs = [
        act,
        pl.BlockSpec((None, HALO_A, D_MODEL), lambda b, t: (b, 0, 0)),
        pl.BlockSpec((None, HALO_B, D_MODEL), lambda b, t: (b, 0, 0)),
        per_layer(1, D_MODEL),
        per_layer(D_MODEL // 2, PROJ_DIM),
        per_layer(CONV_A, D_MODEL),
        per_layer(CONV_B, D_MODEL),
        per_layer(1, D_MODEL),
        per_layer(1, D_MODEL),
        per_layer(1, D_MODEL),
        per_layer(1, D_MODEL),
        per_layer(1, D_MODEL),
        per_layer(C_GROUPS, MLP_CHUNK, MLP_CHUNK),
        per_layer(MLP_CHUNK, D_MODEL),
        per_layer(3, D_MODEL),
        per_layer(3, D_MODEL // 2, D_MODEL),
        per_layer(D_MODEL // 2, D_MODEL),
        pl.BlockSpec((1, D_MODEL), lambda b, t: (0, 0), pipeline_mode=once),
    ]
    out_shape = [
        jax.ShapeDtypeStruct((bsz, seq, D_MODEL), jnp.float32),
        jax.ShapeDtypeStruct((bsz, HALO_A, D_MODEL), jnp.float32),
        jax.ShapeDtypeStruct((bsz, HALO_B, D_MODEL), jnp.float32),
    ]
    out_specs = [
        act,
        pl.BlockSpec((None, HALO_A, D_MODEL), lambda b, t: (b, 0, 0)),
        pl.BlockSpec((None, HALO_B, D_MODEL), lambda b, t: (b, 0, 0)),
    ]
    if emit_v:
        out_shape.append(jax.ShapeDtypeStruct((bsz, seq, D_MODEL), jnp.float32))
        out_specs.append(act)
    f32_tile = lambda n: pltpu.VMEM((tt, n * D_MODEL), jnp.float32)
    bf16_tile = pltpu.VMEM((tt, D_MODEL), jnp.bfloat16)
    scratch_shapes = [
        bf16_tile,
        f32_tile(4), f32_tile(3), f32_tile(3), f32_tile(3),
        f32_tile(1), f32_tile(1), f32_tile(1),
        f32_tile(1), f32_tile(1),
        f32_tile(1),
        pltpu.VMEM((NSLAB, HALO_B + tt, LANES), jnp.float32),
        pltpu.VMEM((NSLAB, HALO_A + tt, LANES), jnp.float32),
        bf16_tile, bf16_tile, bf16_tile, bf16_tile,
        bf16_tile,
    ]
    kern = functools.partial(_layer_kernel, tt=tt, chunk=chunk, emit_v=emit_v,
                             final_norm=final_norm)
    return pl.pallas_call(
        kern,
        grid=grid,
        in_specs=in_specs,
        out_specs=out_specs,
        out_shape=out_shape,
        scratch_shapes=scratch_shapes,
        compiler_params=pltpu.CompilerParams(
            dimension_semantics=("arbitrary", "arbitrary"),
            vmem_limit_bytes=VMEM_LIMIT_BYTES),
        name=f"mixer_layer{layer}_t{tt}",
    )(x, prev_a, prev_b, *params)


def _right_align(prev, halo):
    return jnp.pad(prev, ((0, 0), (halo - prev.shape[1], 0), (0, 0)))


def _pack_row_pairs(w):
    wb = w.astype(jnp.bfloat16)
    k, n = wb.shape[-2:]
    pairs = jnp.swapaxes(wb.reshape(wb.shape[:-2] + (k // 2, 2, n)), -1, -2)
    return lax.bitcast_convert_type(pairs, jnp.uint32)


def kernel(x_prompt, x_sample, cache_conv_a, cache_conv_b, norm_g, w_in, conv_a_w, conv_b_w,
           conv_b_b, ln_b_g, ln_b_b, ln_c_g, ln_c_b, w_s, b_s, gate_b, w_branch, w_out, final_g):
    depth = w_in.shape[0]
    row = lambda p: p.reshape(depth, 1, D_MODEL)
    bsb = jnp.broadcast_to(jnp.swapaxes(b_s, 1, 2)[..., None],
                           (depth, MLP_CHUNK, C_GROUPS, D_MODEL // C_GROUPS))
    params = (row(norm_g), _pack_row_pairs(w_in), conv_a_w, conv_b_w, row(conv_b_b),
              row(ln_b_g), row(ln_b_b), row(ln_c_g), row(ln_c_b), w_s,
              bsb.reshape(depth, MLP_CHUNK, D_MODEL), gate_b, _pack_row_pairs(w_branch),
              _pack_row_pairs(w_out), final_g.reshape(1, D_MODEL))

    def run_group(x, prev_a_layers, prev_b_layers, tt, emit_v):
        new_a, new_b, vs = [], [], []
        for l in range(depth):
            res = _mixer_layer(x, prev_a_layers[l], prev_b_layers[l], params, l, tt=tt,
                               emit_v=emit_v, final_norm=(l == depth - 1))
            x = res[0]
            new_a.append(res[1][:, HALO_A - (CONV_A - 1):])
            new_b.append(res[2][:, HALO_B - (CONV_B - 1):])
            if emit_v:
                vs.append(res[3])
        return x, jnp.stack(new_a), jnp.stack(new_b), vs

    bp = x_prompt.shape[0]
    zero_a = jnp.zeros((bp, HALO_A, D_MODEL), x_prompt.dtype)
    zero_b = jnp.zeros((bp, HALO_B, D_MODEL), x_prompt.dtype)
    y_prompt, pa, pb, _ = run_group(x_prompt, [zero_a] * depth, [zero_b] * depth, 128, False)

    sa = [_right_align(cache_conv_a[l], HALO_A) for l in range(depth)]
    sb = [_right_align(cache_conv_b[l], HALO_B) for l in range(depth)]
    y_sample, na, nb, vs = run_group(x_sample, sa, sb, x_sample.shape[1], True)
    return (y_prompt, y_sample, pa, pb, na, nb, jnp.stack(vs))
```
